```python
import jax
import jax.numpy as jnp
from jax import lax
import numpy as np

D_MODEL = 4096
BATCH = 4
SEQ = 2048
DEPTH = 4
DEC_BATCH = 128
DEC_SEQ = 4
PAST_LEN = 16384
PAGE_SIZE = 128

GDN_QK_HEADS = 16
GDN_DK = 128
GDN_DV = 128
GDN_V_HEADS = D_MODEL // GDN_DV
GDN_QK_W = GDN_QK_HEADS * GDN_DK
GDN_V_W = GDN_V_HEADS * GDN_DV
CONV_W = 4
CONV_CH = 2 * GDN_QK_W + GDN_V_W
GDN_CHUNK = 64
RET_HEADS = 16
RET_DK = D_MODEL // RET_HEADS
RET_DV = D_MODEL // RET_HEADS
RET_QK_W = RET_HEADS * RET_DK
RET_V_W = RET_HEADS * RET_DV
RET_CHUNK = 128
ROPE_BASE = 10000.0
N_MEM = 256
MEM_HEADS = 4
MEM_HD = 256
MEM_W = MEM_HEADS * MEM_HD
D_FF = 4 * D_MODEL
EPS = 1e-6
IN_SPLITS = (GDN_QK_W, GDN_QK_W, GDN_V_W, GDN_V_W, GDN_V_HEADS, GDN_V_HEADS,
             RET_QK_W, RET_QK_W, RET_V_W, RET_V_W, D_MODEL, D_MODEL)
IN_COLS = sum(IN_SPLITS)

kernel_name = 'gdn_retention_memory_hybrid_step'


def _rmsnorm(x, g):
    xf = x.astype(jnp.float32)
    y = xf * lax.rsqrt(jnp.mean(xf * xf, axis=-1, keepdims=True) + EPS)
    return (y * g.astype(jnp.float32)).astype(x.dtype)


def _l2norm(x):
    return x * lax.rsqrt(jnp.sum(x * x, axis=-1, keepdims=True) + EPS)


def _split_cols(p):
    idx = np.cumsum(IN_SPLITS)[:-1].tolist()
    return jnp.split(p, idx, axis=-1)


def _causal_conv(u, buf, w):
    t = u.shape[1]
    full = jnp.concatenate([buf.astype(u.dtype), u], axis=1)
    acc = full[:, 0:t] * w[0]
    for i in range(1, CONV_W):
        acc = acc + full[:, i:i + t] * w[i]
    return jax.nn.silu(acc), full[:, t:]


def _rotary(x, pos):
    half = x.shape[-1] // 2
    inv = 1.0 / (ROPE_BASE ** jnp.linspace(0.0, 1.0, half, dtype=jnp.float32))
    ang = pos.astype(jnp.float32)[:, None] * inv[None, :]
    cos = jnp.cos(ang)[None, :, None, :]
    sin = jnp.sin(ang)[None, :, None, :]
    x1, x2 = x[..., :half], x[..., half:]
    return jnp.concatenate([x1 * cos - x2 * sin, x1 * sin + x2 * cos], axis=-1)


def _to_chunks(a, chunk):
    b, t = a.shape[:2]
    a = a.reshape((b, t // chunk, chunk) + a.shape[2:])
    return jnp.swapaxes(jnp.swapaxes(a, 0, 1), 2, 3)


def _from_chunks(o):
    n, b, h, c, d = o.shape
    return jnp.swapaxes(jnp.swapaxes(o, 2, 3), 0, 1).reshape(b, n * c, h, d)


def _gated_delta(q, k, v, g, beta, s0, chunk):
    q, k, v, g, beta = (_to_chunks(a, chunk) for a in (q, k, v, g, beta))
    gc = jnp.cumsum(g, axis=-1)
    tri = jnp.tril(jnp.ones((chunk, chunk), dtype=bool))
    strict = jnp.tril(jnp.ones((chunk, chunk), dtype=bool), -1)
    decay = jnp.exp(jnp.where(tri, gc[..., :, None] - gc[..., None, :], -jnp.inf))
    kb = k * beta[..., None]
    lmat = jnp.where(strict, jnp.einsum('nbhid,nbhjd->nbhij', kb, k) * decay, 0.0)
    eye = jnp.eye(chunk, dtype=jnp.float32)
    tinv = lax.linalg.triangular_solve(lmat + eye, jnp.broadcast_to(eye, lmat.shape),
                                       left_side=True, lower=True, unit_diagonal=True)
    u = tinv @ (v * beta[..., None])
    w = tinv @ (kb * jnp.exp(gc)[..., None])
    qk = jnp.einsum('nbhid,nbhjd->nbhij', q, k) * decay

    def step(s, xs):
        q_i, k_i, u_i, w_i, gc_i, qk_i = xs
        v_new = u_i - w_i @ s
        o = (q_i * jnp.exp(gc_i)[..., None]) @ s + qk_i @ v_new
        g_last = gc_i[..., -1:]
        k_dec = k_i * jnp.exp(g_last - gc_i)[..., None]
        s = s * jnp.exp(g_last)[..., None] + jnp.einsum('bhcd,bhce->bhde', k_dec, v_new)
        return s, o

    s, o = lax.scan(step, s0, (q, k, u, w, gc, qk))
    return _from_chunks(o), s


def _retention(q, k, v, s0, chunk):
    h = q.shape[2]
    q, k, v = (_to_chunks(a, chunk) for a in (q, k, v))
    log_gamma = jnp.log1p(-jnp.exp2(-5.0 - jnp.arange(h, dtype=jnp.float32)))
    idx = jnp.arange(chunk, dtype=jnp.float32)
    rel = idx[:, None] - idx[None, :]
    dmask = jnp.exp(jnp.where(rel >= 0, log_gamma[:, None, None] * rel, -jnp.inf))
    inner = jnp.einsum('nbhid,nbhjd->nbhij', q, k) * dmask
    q_dec = jnp.exp(log_gamma[:, None] * (idx + 1.0))[..., None]
    k_dec = jnp.exp(log_gamma[:, None] * (chunk - 1.0 - idx))[..., None]
    c_dec = jnp.exp(log_gamma * chunk)[:, None, None]

    def step(s, xs):
        q_i, k_i, v_i, a_i = xs
        o = a_i @ v_i + (q_i @ s) * q_dec
        s = s * c_dec + jnp.einsum('bhcd,bhce->bhde', k_i * k_dec, v_i)
        return s, o

    s, o = lax.scan(step, s0, (q, k, v, inner))
    return _from_chunks(o), s


def _mixer(xn, conv_buf, s_gdn, s_ret, pos, chunk_gdn, chunk_ret,
           w_in, conv_w, a_log, dt_bias, gdn_norm, ret_norm, w_out):
    f32 = jnp.float32
    b, t, _ = xn.shape
    qa, ka, va, za, ba, aa, qb, kb, vb, gb, gate_a, gate_b = _split_cols(xn @ w_in)
    conv_out, new_buf = _causal_conv(jnp.concatenate([qa, ka, va], axis=-1), conv_buf, conv_w)
    qa, ka, va = jnp.split(conv_out.astype(f32), [GDN_QK_W, 2 * GDN_QK_W], axis=-1)
    rep = GDN_V_HEADS // GDN_QK_HEADS
    qa = jnp.repeat(_l2norm(qa.reshape(b, t, GDN_QK_HEADS, GDN_DK)) * GDN_DK ** -0.5, rep, axis=2)
    ka = jnp.repeat(_l2norm(ka.reshape(b, t, GDN_QK_HEADS, GDN_DK)), rep, axis=2)
    va = va.reshape(b, t, GDN_V_HEADS, GDN_DV)
    beta = jax.nn.sigmoid(ba.astype(f32))
    g = -jnp.exp(a_log.astype(f32)) * jax.nn.softplus(aa.astype(f32) + dt_bias.astype(f32))
    o_a, s_gdn = _gated_delta(qa, ka, va, g, beta, s_gdn.astype(f32), chunk_gdn)
    o_a = _rmsnorm(o_a, gdn_norm) * jax.nn.silu(za.astype(f32).reshape(b, t, GDN_V_HEADS, GDN_DV))
    o_a = o_a.reshape(b, t, D_MODEL).astype(xn.dtype)
    qb = _rotary(qb.astype(f32).reshape(b, t, RET_HEADS, RET_DK), pos)
    kb = _rotary(kb.astype(f32).reshape(b, t, RET_HEADS, RET_DK), pos) * RET_DK ** -0.5
    vb = vb.astype(f32).reshape(b, t, RET_HEADS, RET_DV)
    o_b, s_ret = _retention(qb, kb, vb, s_ret.astype(f32), chunk_ret)
    o_b = _rmsnorm(o_b, ret_norm.reshape(RET_HEADS, RET_DV)) * jax.nn.silu(gb.astype(f32).reshape(b, t, RET_HEADS, RET_DV))
    o_b = o_b.reshape(b, t, D_MODEL).astype(xn.dtype)
    merged = jax.nn.sigmoid(gate_a) * o_a + jax.nn.sigmoid(gate_b) * o_b
    return merged @ w_out, new_buf, s_gdn, s_ret


def _mem_kv(mem, g, w_k, w_v):
    b = mem.shape[0]
    mn = _rmsnorm(mem, g)
    return ((mn @ w_k).reshape(b, N_MEM, MEM_HEADS, MEM_HD),
            (mn @ w_v).reshape(b, N_MEM, MEM_HEADS, MEM_HD))


def _mem_attn(hn, mk, mv, w_q, w_o):
    b, t, _ = hn.shape
    q = (hn @ w_q).reshape(b, t, MEM_HEADS, MEM_HD).astype(jnp.float32)
    s = jnp.einsum('bthd,bmhd->bhtm', q, mk.astype(jnp.float32)) * MEM_HD ** -0.5
    p = jax.nn.softmax(s, axis=-1)
    o = jnp.einsum('bhtm,bmhd->bthd', p, mv.astype(jnp.float32)).reshape(b, t, MEM_W).astype(hn.dtype)
    return o @ w_o


def _layer(h, mk, mv, conv_buf, s_gdn, s_ret, pos, chunk_gdn, chunk_ret,
           norm_mix, w_in, conv_w, a_log, dt_bias, gdn_norm, ret_norm, w_out,
           norm_mem_q, w_mq, w_mo, norm_ffn, w_ff1, w_ff2):
    mix, conv_buf, s_gdn, s_ret = _mixer(_rmsnorm(h, norm_mix), conv_buf, s_gdn, s_ret, pos,
                                         chunk_gdn, chunk_ret, w_in, conv_w, a_log, dt_bias,
                                         gdn_norm, ret_norm, w_out)
    h = h + mix
    h = h + _mem_attn(_rmsnorm(h, norm_mem_q), mk, mv, w_mq, w_mo)
    u = jax.nn.relu(_rmsnorm(h, norm_ffn) @ w_ff1)
    h = h + (u * u) @ w_ff2
    return h, conv_buf, s_gdn, s_ret


def setup_inputs(seed: int = 0) -> dict:
    key = jax.random.key(seed)
    ks = jax.random.split(key, 26)
    f32 = jnp.float32
    nrm = lambda k, shape, scale: jax.random.normal(k, shape, f32) * scale
    gain = lambda k, shape: 1.0 + 0.02 * jax.random.normal(k, shape, f32)
    dt = jnp.exp(jax.random.uniform(ks[10], (DEPTH, GDN_V_HEADS), f32, np.log(1e-3), np.log(1e-1)))
    return {
        'x_prompt': nrm(ks[0], (BATCH, SEQ, D_MODEL), 1.0),
        'x_sample': nrm(ks[1], (DEC_BATCH, DEC_SEQ, D_MODEL), 1.0),
        'state_gdn': nrm(ks[2], (DEPTH, DEC_BATCH, GDN_V_HEADS, GDN_DK, GDN_DV), 0.5),
        'state_conv': nrm(ks[3], (DEPTH, DEC_BATCH, CONV_W - 1, CONV_CH), 1.0),
        'state_ret': nrm(ks[4], (DEPTH, DEC_BATCH, RET_HEADS, RET_DK, RET_DV), 1.0),
        'cache_mem_k': nrm(ks[5], (DEPTH, DEC_BATCH, N_MEM, MEM_HEADS, MEM_HD), 1.0),
        'cache_mem_v': nrm(ks[6], (DEPTH, DEC_BATCH, N_MEM, MEM_HEADS, MEM_HD), 1.0),
        'mem_prompt': nrm(ks[7], (BATCH, N_MEM, D_MODEL), 1.0),
        'norm_mix': gain(ks[8], (DEPTH, D_MODEL)),
        'w_in': nrm(ks[9], (DEPTH, D_MODEL, IN_COLS), D_MODEL ** -0.5),
        'conv_w': nrm(ks[11], (DEPTH, CONV_W, CONV_CH), CONV_W ** -0.5),
        'a_log': jnp.log(jax.random.uniform(ks[12], (DEPTH, GDN_V_HEADS), f32, 1.0, 16.0)),
        'dt_bias': dt + jnp.log(-jnp.expm1(-dt)),
        'gdn_norm': gain(ks[13], (DEPTH, GDN_DV)),
        'ret_norm': gain(ks[14], (DEPTH, RET_V_W)),
        'w_out': nrm(ks[15], (DEPTH, D_MODEL, D_MODEL), D_MODEL ** -0.5),
        'norm_mem_q': gain(ks[16], (DEPTH, D_MODEL)),
        'norm_mem_kv': gain(ks[17], (DEPTH, D_MODEL)),
        'w_mq': nrm(ks[18], (DEPTH, D_MODEL, MEM_W), D_MODEL ** -0.5),
        'w_mk': nrm(ks[19], (DEPTH, D_MODEL, MEM_W), D_MODEL ** -0.5),
        'w_mv': nrm(ks[20], (DEPTH, D_MODEL, MEM_W), D_MODEL ** -0.5),
        'w_mo': nrm(ks[21], (DEPTH, MEM_W, D_MODEL), MEM_W ** -0.5),
        'norm_ffn': gain(ks[22], (DEPTH, D_MODEL)),
        'w_ff1': nrm(ks[23], (DEPTH, D_MODEL, D_FF), D_MODEL ** -0.5),
        'w_ff2': nrm(ks[24], (DEPTH, D_FF, D_MODEL), D_FF ** -0.5),
        'norm_final': gain(ks[25], (D_MODEL,)),
    }


def reference(x_prompt, x_sample, state_gdn, state_conv, state_ret, cache_mem_k, cache_mem_v,
              mem_prompt, norm_mix, w_in, conv_w, a_log, dt_bias, gdn_norm, ret_norm, w_out,
              norm_mem_q, norm_mem_kv, w_mq, w_mk, w_mv, w_mo, norm_ffn, w_ff1, w_ff2, norm_final):
    f32 = jnp.float32
    dtp = x_prompt.dtype
    pos_p = jnp.arange(SEQ, dtype=jnp.int32)
    pos_s = PAST_LEN + jnp.arange(DEC_SEQ, dtype=jnp.int32)
    zero_buf = jnp.zeros((BATCH, CONV_W - 1, CONV_CH), dtp)
    zero_gdn = jnp.zeros((BATCH, GDN_V_HEADS, GDN_DK, GDN_DV), f32)
    zero_ret = jnp.zeros((BATCH, RET_HEADS, RET_DK, RET_DV), f32)
    hp, hs = x_prompt, x_sample
    pg, pc, pr, pk, pv, sg, sc, sr = [], [], [], [], [], [], [], []
    for l in range(DEPTH):
        lw = (norm_mix[l], w_in[l], conv_w[l], a_log[l], dt_bias[l], gdn_norm[l], ret_norm[l], w_out[l],
              norm_mem_q[l], w_mq[l], w_mo[l], norm_ffn[l], w_ff1[l], w_ff2[l])
        mk_p, mv_p = _mem_kv(mem_prompt, norm_mem_kv[l], w_mk[l], w_mv[l])
        hp, cb_p, g_p, r_p = _layer(hp, mk_p, mv_p, zero_buf, zero_gdn, zero_ret, pos_p,
                                    GDN_CHUNK, RET_CHUNK, *lw)
        hs, cb_s, g_s, r_s = _layer(hs, cache_mem_k[l], cache_mem_v[l], state_conv[l], state_gdn[l],
                                    state_ret[l], pos_s, DEC_SEQ, DEC_SEQ, *lw)
        pg.append(g_p.astype(dtp))
        pc.append(cb_p.astype(dtp))
        pr.append(r_p.astype(dtp))
        pk.append(mk_p)
        pv.append(mv_p)
        sg.append(g_s.astype(state_gdn.dtype))
        sc.append(cb_s.astype(state_conv.dtype))
        sr.append(r_s.astype(state_ret.dtype))
    y_prompt = _rmsnorm(hp, norm_final)
    y_sample = _rmsnorm(hs, norm_final)
    p_state_gdn = jnp.stack(pg)
    p_state_conv = jnp.stack(pc)
    p_state_ret = jnp.stack(pr)
    p_cache_mem_k = jnp.stack(pk)
    p_cache_mem_v = jnp.stack(pv)
    s_state_gdn = jnp.stack(sg)
    s_state_conv = jnp.stack(sc)
    s_state_ret = jnp.stack(sr)
    return (y_prompt, y_sample, p_state_gdn, p_state_conv, p_state_ret, p_cache_mem_k, p_cache_mem_v,
            s_state_gdn, s_state_conv, s_state_ret)
```

```python
import functools

import jax
import jax.numpy as jnp
from jax import lax
from jax.experimental import pallas as pl
from jax.experimental.pallas import tpu as pltpu

F32 = jnp.float32
BF16 = jnp.bfloat16

EPS = 1e-6
GDN_CHUNK = 64
RET_CHUNK = 128
PAST_LEN = 16384
ROPE_BASE = 10000.0
CONV_W = 4
SUBLANES = 8
CONV_HIST = CONV_W - 1
GDN_HEAD_GROUP = 4
RET_HEAD_GROUP = 4
SAMPLE_HEAD_GROUP = 4
VMEM_LIMIT = 56 * 1024 * 1024


def _params(*sem):
    return pltpu.CompilerParams(dimension_semantics=sem, vmem_limit_bytes=VMEM_LIMIT)


def _pick_tile(n, cap, mult=SUBLANES):
    best = None
    for t in range(mult, min(n, cap) + 1, mult):
        if n % t == 0:
            best = t
    assert best is not None, (n, cap)
    return best


def _dot(a, b):
    return jnp.dot(a.astype(BF16), b.astype(BF16), preferred_element_type=F32)


def _dot_nt(a, b):
    return lax.dot_general(a.astype(BF16), b.astype(BF16), (((1,), (1,)), ((), ())),
                           preferred_element_type=F32)


def _dot_tn(a, b):
    return lax.dot_general(a.astype(BF16), b.astype(BF16), (((0,), (0,)), ((), ())),
                           preferred_element_type=F32)


def _split_bf16(a):
    hi = a.astype(BF16)
    lo = (a - hi.astype(F32)).astype(BF16)
    return hi, lo


def _dot_3pass(a, b):
    ah, al = _split_bf16(a)
    bh, bl = _split_bf16(b)
    d = functools.partial(jnp.dot, preferred_element_type=F32)
    return d(ah, bh) + (d(ah, bl) + d(al, bh))


def _sigmoid(x):
    return 1.0 / (1.0 + jnp.exp(-x))


def _silu(x):
    return x * _sigmoid(x)


def _softplus(x):
    return jnp.maximum(x, 0.0) + jnp.log(1.0 + jnp.exp(-jnp.abs(x)))


def _iota2(shape, dim):
    return lax.broadcasted_iota(jnp.int32, shape, dim)


def _pick_col(x, idx):
    lane = _iota2(x.shape, 1)
    return jnp.sum(jnp.where(lane == idx, x, 0.0), axis=1, keepdims=True)


def _tiny_mm(a, b):
    r = a.shape[1]
    lane = _iota2(a.shape, 1)
    out = None
    for j in range(r):
        col = jnp.sum(jnp.where(lane == j, a, 0.0), axis=1, keepdims=True)
        term = col * b[j:j + 1, :]
        out = term if out is None else out + term
    return out


def _tiny_nt(a, b):
    r = b.shape[0]
    lane = _iota2((a.shape[0], r), 1)
    out = jnp.zeros((a.shape[0], r), F32)
    for j in range(r):
        col = jnp.sum(a * b[j:j + 1, :], axis=1, keepdims=True)
        out = jnp.where(lane == j, col, out)
    return out


def _pad_rows_tn(a, b, rows):
    pad = rows - a.shape[0]
    a = jnp.concatenate([a, jnp.zeros((pad, a.shape[1]), F32)], axis=0)
    b = jnp.concatenate([b, jnp.zeros((pad, b.shape[1]), F32)], axis=0)
    return _dot_tn(a, b)


def _unit_lower_inv(lmats, tiny):
    c = lmats[0].shape[0]
    mm = _tiny_mm if tiny else _dot_3pass
    eye = (_iota2((c, c), 0) == _iota2((c, c), 1)).astype(F32)
    ts = [eye - l for l in lmats]
    ps = list(lmats)
    n = 1
    while 2 * n < c:
        ps = [mm(p, p) for p in ps]
        ts = [t + mm(t, p) for t, p in zip(ts, ps)]
        n *= 2
    return ts


def _gdn_chunks(qs, ks, vs, gs, betas, ss, kks, qk_raws, tiny):
    n = len(qs)
    c = qs[0].shape[0]
    ii = _iota2((c, c), 0)
    jj = _iota2((c, c), 1)
    tri = ii >= jj
    g_rows = [jnp.sum(jnp.where(ii == jj, g, 0.0), axis=0, keepdims=True) for g in gs]
    gcs = [jnp.sum(jnp.where(tri, gr, 0.0), axis=1, keepdims=True) for gr in g_rows]
    gc_rows = [jnp.sum(jnp.where(ii <= jj, g, 0.0), axis=0, keepdims=True) for g in gs]
    decays = [jnp.exp(jnp.where(tri, gc - gr, -jnp.inf)) for gc, gr in zip(gcs, gc_rows)]
    lmats = [jnp.where(ii > jj, betas[i] * kks[i] * decays[i], 0.0) for i in range(n)]
    tinvs = _unit_lower_inv(lmats, tiny)
    mm = _tiny_mm if tiny else _dot
    egcs = [jnp.exp(gc) for gc in gcs]
    us = [mm(tinvs[i], vs[i] * betas[i]) for i in range(n)]
    ws = [mm(tinvs[i], ks[i] * (betas[i] * egcs[i])) for i in range(n)]
    wss = [_dot(ws[i], ss[i]) for i in range(n)]
    qss = [_dot(qs[i] * egcs[i], ss[i]) for i in range(n)]
    v_news = [us[i] - wss[i] for i in range(n)]
    os_ = [qss[i] + mm(qk_raws[i] * decays[i], v_news[i]) for i in range(n)]
    g_lasts = [jnp.sum(g, axis=0, keepdims=True) for g in gs]
    k_decs = [ks[i] * jnp.exp(g_lasts[i] - gcs[i]) for i in range(n)]
    if tiny:
        upds = [_pad_rows_tn(k_decs[i], v_news[i], ss[i].shape[0]) for i in range(n)]
    else:
        upds = [_dot_tn(k_decs[i], v_news[i]) for i in range(n)]
    return os_, [ss[i] * jnp.exp(g_lasts[i]) + upds[i] for i in range(n)]


def _ret_chunks(qs, ks, vs, ss, lgs, c_eff, tiny):
    c = qs[0].shape[0]
    rel = (_iota2((c, c), 0) - _iota2((c, c), 1)).astype(F32)
    idx = _iota2((c, 1), 0).astype(F32)
    n = len(qs)
    dmasks = [jnp.exp(jnp.where(rel >= 0, lg * rel, -jnp.inf)) for lg in lgs]
    q_decs = [jnp.exp(lg * (idx + 1.0)) for lg in lgs]
    k_decs = [jnp.exp(lg * (c_eff - 1.0 - idx)) for lg in lgs]
    c_decs = [jnp.exp(lg * float(c_eff)) for lg in lgs]
    if tiny:
        inners = [_tiny_nt(qs[i], ks[i]) * dmasks[i] for i in range(n)]
        qss = [_dot(qs[i], ss[i]) for i in range(n)]
        os_ = [_tiny_mm(inners[i], vs[i]) + qss[i] * q_decs[i] for i in range(n)]
        upds = [_pad_rows_tn(ks[i] * k_decs[i], vs[i], 128) for i in range(n)]
    else:
        inners = [_dot_nt(qs[i], ks[i]) * dmasks[i] for i in range(n)]
        qss = [_dot(qs[i], ss[i]) for i in range(n)]
        os_ = [_dot(inners[i], vs[i]) + qss[i] * q_decs[i] for i in range(n)]
        upds = [_dot_tn(ks[i] * k_decs[i], vs[i]) for i in range(n)]
    return os_, [ss[i] * c_decs[i] + upds[i] for i in range(n)]


def _rotary(x, cos, sin):
    half = x.shape[-1] // 2
    x1 = x[:, :half]
    x2 = x[:, half:]
    return jnp.concatenate([x1 * cos - x2 * sin, x1 * sin + x2 * cos], axis=-1)


def _conv_silu(xs_ref, w, rows, lead=()):
    base = SUBLANES - CONV_HIST
    acc = None
    for i in range(CONV_W):
        term = xs_ref[lead + (pl.ds(base + i, rows), slice(None))] * w[i:i + 1, :]
        acc = term if acc is None else acc + term
    return _silu(acc)


def _l2norm(x):
    return x * lax.rsqrt(jnp.sum(x * x, axis=-1, keepdims=True) + EPS)


def _head_rmsnorm(o, gain):
    return o * lax.rsqrt(jnp.mean(o * o, axis=-1, keepdims=True) + EPS) * gain


def _rmsnorm_kernel(x_ref, g_ref, o_ref):
    x = x_ref[...]
    y = x * lax.rsqrt(jnp.mean(x * x, axis=-1, keepdims=True) + EPS)
    o_ref[...] = (y * g_ref[...]).astype(o_ref.dtype)


def rmsnorm(x, gain, out_dtype):
    m, d = x.shape
    tm = _pick_tile(m, 544)
    return pl.pallas_call(
        _rmsnorm_kernel,
        grid=(m // tm,),
        in_specs=[pl.BlockSpec((tm, d), lambda i: (i, 0)),
                  pl.BlockSpec((1, d), lambda i: (0, 0))],
        out_specs=pl.BlockSpec((tm, d), lambda i: (i, 0)),
        out_shape=jax.ShapeDtypeStruct((m, d), out_dtype),
        compiler_params=_params("parallel"),
        name="rmsnorm",
    )(x, gain.reshape(1, d))


def _mm_kernel(*refs, nk, epilogue):
    if epilogue == "residual":
        x_ref, w_ref, r_ref, o_ref = refs
    else:
        x_ref, w_ref, o_ref = refs
    acc = jnp.dot(x_ref[...], w_ref[0].astype(BF16), preferred_element_type=F32)

    def finish(a):
        if epilogue == "relu2":
            a = jnp.maximum(a, 0.0)
            a = a * a
        elif epilogue == "residual":
            a = r_ref[...] + a
        return a.astype(o_ref.dtype)

    if nk == 1:
        o_ref[...] = finish(acc)
    else:
        kstep = pl.program_id(2)

        @pl.when(kstep == 0)
        def _():
            o_ref[...] = acc

        @pl.when(jnp.logical_and(kstep > 0, kstep < nk - 1))
        def _():
            o_ref[...] += acc

        @pl.when(kstep == nk - 1)
        def _():
            o_ref[...] = finish(o_ref[...] + acc)


def matmul(x, w, layer, *, col0=0, ncols=None, out_dtype=F32, epilogue="none", residual=None,
           tm_cap=1088, tn_cap=512, tk_cap=4096):
    m, kdim = x.shape
    assert w.shape[1] == kdim
    ncols = w.shape[2] - col0 if ncols is None else ncols
    tm = _pick_tile(m, tm_cap)
    tk = _pick_tile(kdim, tk_cap, 128)
    nk = kdim // tk
    if ncols % 128 == 0:
        tn = _pick_tile(ncols, tn_cap, 128)
        assert col0 % tn == 0
    else:
        tn = ncols
        assert col0 == 0 and w.shape[2] == ncols
    assert nk == 1 or out_dtype == F32
    joff = col0 // tn
    in_specs = [pl.BlockSpec((tm, tk), lambda i, j, k: (i, k)),
                pl.BlockSpec((1, tk, tn), lambda i, j, k: (layer, k, j + joff))]
    args = [x, w]
    if epilogue == "residual":
        in_specs.append(pl.BlockSpec((tm, tn), lambda i, j, k: (i, j)))
        args.append(residual)
    return pl.pallas_call(
        functools.partial(_mm_kernel, nk=nk, epilogue=epilogue),
        grid=(m // tm, ncols // tn, nk),
        in_specs=in_specs,
        out_specs=pl.BlockSpec((tm, tn), lambda i, j, k: (i, j)),
        out_shape=jax.ShapeDtypeStruct((m, ncols), out_dtype),
        compiler_params=_params("parallel", "parallel", "arbitrary"),
        name="matmul_" + epilogue,
    )(*args)


def _gdn_gates(ba, alog, dtb, nvh, col):
    beta = _sigmoid(_pick_col(ba, col))
    a_raw = _pick_col(ba, nvh + col)
    g = -jnp.exp(_pick_col(alog, col)) * _softplus(a_raw + _pick_col(dtb, col))
    return beta, g


def _gdn_prompt_kernel(q_ref, k_ref, v_ref, z_ref, ba_ref, ga_ref, cwq_ref, cwk_ref, cwv_ref,
                       alog_ref, dtb_ref, gn_ref, o_ref, sout_ref, s_scr, xq, xk, xv, *, nvh, rep, group):
    j = pl.program_id(1)
    c = pl.program_id(2)
    rows = q_ref.shape[0]
    dk = q_ref.shape[1] // group
    dv = v_ref.shape[1] // (group * rep)
    hist = pl.ds(SUBLANES - CONV_HIST, CONV_HIST)
    tail = pl.ds(SUBLANES + rows - CONV_HIST, CONV_HIST)

    @pl.when(c == 0)
    def _():
        s_scr[...] = jnp.zeros_like(s_scr)
        for xs in (xq, xk, xv):
            xs[pl.ds(0, SUBLANES), :] = jnp.zeros((SUBLANES, xs.shape[1]), F32)

    @pl.when(c > 0)
    def _():
        for xs in (xq, xk, xv):
            xs[hist, :] = xs[tail, :]

    xq[pl.ds(SUBLANES, rows), :] = q_ref[...]
    xk[pl.ds(SUBLANES, rows), :] = k_ref[...]
    xv[pl.ds(SUBLANES, rows), :] = v_ref[...]
    qc = _conv_silu(xq, cwq_ref[0], rows)
    kc = _conv_silu(xk, cwk_ref[0], rows)
    vc = _conv_silu(xv, cwv_ref[0], rows)
    ba = ba_ref[...]
    nv = group * rep
    qh = [_l2norm(qc[:, hq * dk:(hq + 1) * dk]) * (dk ** -0.5) for hq in range(group)]
    kh = [_l2norm(kc[:, hq * dk:(hq + 1) * dk]) for hq in range(group)]
    kk = [_dot_nt(k, k) for k in kh]
    qk_raw = [_dot_nt(q, k) for q, k in zip(qh, kh)]
    gates = [_gdn_gates(ba, alog_ref[...], dtb_ref[...], nvh, j * nv + hv) for hv in range(nv)]
    lanes = [slice(hv * dv, (hv + 1) * dv) for hv in range(nv)]
    per_v = lambda xs: [xs[hv // rep] for hv in range(nv)]
    os_, s_new = _gdn_chunks(per_v(qh), per_v(kh), [vc[:, ln] for ln in lanes], [g for _, g in gates],
                             [b for b, _ in gates], [s_scr[hv] for hv in range(nv)], per_v(kk),
                             per_v(qk_raw), tiny=False)
    for hv in range(nv):
        s_scr[hv] = s_new[hv]
        y = _head_rmsnorm(os_[hv], gn_ref[...]) * _silu(z_ref[:, lanes[hv]])
        o_ref[:, lanes[hv]] = y * _sigmoid(ga_ref[:, lanes[hv]])

    @pl.when(c == pl.num_programs(2) - 1)
    def _():
        sout_ref[0] = s_scr[...]


def gdn_prompt(pa, pba, pb, conv_w, a_log, dt_bias, gdn_norm, layer, dims):
    b, t = dims["B"], dims["T"]
    nqk, nvh, dk, dv = dims["GDN_QK_HEADS"], dims["GDN_V_HEADS"], dims["GDN_DK"], dims["GDN_DV"]
    rep = nvh // nqk
    group = _pick_tile(nqk, GDN_HEAD_GROUP, 1)
    ngroups = nqk // group
    chunk = GDN_CHUNK
    nc = t // chunk
    qw = group * dk
    vw = group * rep * dv
    qk_w = nqk * dk
    v_blk0 = 2 * qk_w // vw
    z_blk0 = (2 * qk_w + nvh * dv) // vw
    ga_blk0 = dims["GATE_A_COL"] // vw
    nba = pba.shape[1]
    row = lambda bi, j, c: bi * nc + c
    in_specs = [
        pl.BlockSpec((chunk, qw), lambda bi, j, c: (row(bi, j, c), j)),
        pl.BlockSpec((chunk, qw), lambda bi, j, c: (row(bi, j, c), ngroups + j)),
        pl.BlockSpec((chunk, vw), lambda bi, j, c: (row(bi, j, c), v_blk0 + j)),
        pl.BlockSpec((chunk, vw), lambda bi, j, c: (row(bi, j, c), z_blk0 + j)),
        pl.BlockSpec((chunk, nba), lambda bi, j, c: (row(bi, j, c), 0)),
        pl.BlockSpec((chunk, vw), lambda bi, j, c: (row(bi, j, c), ga_blk0 + j)),
        pl.BlockSpec((1, CONV_W, qw), lambda bi, j, c: (layer, 0, j)),
        pl.BlockSpec((1, CONV_W, qw), lambda bi, j, c: (layer, 0, ngroups + j)),
        pl.BlockSpec((1, CONV_W, vw), lambda bi, j, c: (layer, 0, v_blk0 + j)),
        pl.BlockSpec((1, nvh), lambda bi, j, c: (0, 0)),
        pl.BlockSpec((1, nvh), lambda bi, j, c: (0, 0)),
        pl.BlockSpec((1, dv), lambda bi, j, c: (0, 0)),
    ]
    out_specs = [
        pl.BlockSpec((chunk, vw), lambda bi, j, c: (row(bi, j, c), j)),
        pl.BlockSpec((1, group * rep, dk, dv), lambda bi, j, c: (bi, j, 0, 0)),
    ]
    out_shape = [jax.ShapeDtypeStruct((b * t, nvh * dv), F32),
                 jax.ShapeDtypeStruct((b, nvh, dk, dv), F32)]
    xrows = SUBLANES + chunk
    return pl.pallas_call(
        functools.partial(_gdn_prompt_kernel, nvh=nvh, rep=rep, group=group),
        grid=(b, ngroups, nc),
        in_specs=in_specs,
        out_specs=out_specs,
        out_shape=out_shape,
        scratch_shapes=[pltpu.VMEM((group * rep, dk, dv), F32), pltpu.VMEM((xrows, qw), F32),
                        pltpu.VMEM((xrows, qw), F32), pltpu.VMEM((xrows, vw), F32)],
        compiler_params=_params("parallel", "parallel", "arbitrary"),
        name="gdn_prompt",
    )(pa, pa, pa, pa, pba, pb, conv_w, conv_w, conv_w,
      a_log[layer][None], dt_bias[layer][None], gdn_norm[layer][None])


def _ret_prompt_kernel(lg_ref, q_ref, k_ref, v_ref, g_ref, gb_ref, ma_ref, cos_ref, sin_ref, rn_ref,
                       o_ref, sout_ref, s_scr, *, group):
    hg = pl.program_id(1)
    c = pl.program_id(2)
    rows = q_ref.shape[0]
    dk = q_ref.shape[1] // group

    @pl.when(c == 0)
    def _():
        s_scr[...] = jnp.zeros_like(s_scr)

    cos = cos_ref[...]
    sin = sin_ref[...]
    lanes = [slice(i * dk, (i + 1) * dk) for i in range(group)]
    qs = [_rotary(q_ref[:, ln], cos, sin) for ln in lanes]
    ks = [_rotary(k_ref[:, ln], cos, sin) * (dk ** -0.5) for ln in lanes]
    lgs = [jnp.full((1, 1), lg_ref[hg * group + i], F32) for i in range(group)]
    os_, s_new = _ret_chunks(qs, ks, [v_ref[:, ln] for ln in lanes], [s_scr[i] for i in range(group)],
                             lgs, rows, tiny=False)
    for i, ln in enumerate(lanes):
        s_scr[i] = s_new[i]
        y = _head_rmsnorm(os_[i], rn_ref[:, ln]) * _silu(g_ref[:, ln])
        o_ref[:, ln] = (ma_ref[:, ln] + _sigmoid(gb_ref[:, ln]) * y).astype(o_ref.dtype)

    @pl.when(c == pl.num_programs(2) - 1)
    def _():
        sout_ref[0] = s_scr[...]


def ret_prompt(pb, ma, cos, sin, lg, ret_norm, layer, dims):
    b, t = dims["B"], dims["T"]
    nh, dk, dv = dims["RET_HEADS"], dims["RET_DK"], dims["RET_DV"]
    assert dk == dv
    chunk = RET_CHUNK
    nc = t // chunk
    group = _pick_tile(nh, RET_HEAD_GROUP, 1)
    ng = nh // group
    gw = group * dk
    gb_blk0 = dims["GATE_B_COL"] // gw
    row = lambda bi, h, c: bi * nc + c
    blk = lambda off: pl.BlockSpec((chunk, gw), lambda bi, h, c: (row(bi, h, c), off + h))
    in_specs = [
        pl.BlockSpec(memory_space=pltpu.SMEM),
        blk(0), blk(ng), blk(2 * ng), blk(3 * ng), blk(gb_blk0),
        pl.BlockSpec((chunk, gw), lambda bi, h, c: (row(bi, h, c), h)),
        pl.BlockSpec((chunk, dk // 2), lambda bi, h, c: (c, 0)),
        pl.BlockSpec((chunk, dk // 2), lambda bi, h, c: (c, 0)),
        pl.BlockSpec((1, gw), lambda bi, h, c: (0, h)),
    ]
    out_specs = [
        pl.BlockSpec((chunk, gw), lambda bi, h, c: (row(bi, h, c), h)),
        pl.BlockSpec((1, group, dk, dv), lambda bi, h, c: (bi, h, 0, 0)),
    ]
    out_shape = [jax.ShapeDtypeStruct((b * t, nh * dv), BF16),
                 jax.ShapeDtypeStruct((b, nh, dk, dv), F32)]
    return pl.pallas_call(
        functools.partial(_ret_prompt_kernel, group=group),
        grid=(b, ng, nc),
        in_specs=in_specs,
        out_specs=out_specs,
        out_shape=out_shape,
        scratch_shapes=[pltpu.VMEM((group, dk, dv), F32)],
        compiler_params=_params("parallel", "parallel", "arbitrary"),
        name="ret_prompt",
    )(lg, pb, pb, pb, pb, pb, ma, cos, sin, ret_norm[layer][None])


def _gdn_sample_kernel(xs_ref, z_ref, ga_ref, ba_ref, cw_ref, alog_ref, dtb_ref, gn_ref, s_ref,
                       o_ref, sout_ref, *, nqk, nvh, tokens):
    rep = nvh // nqk
    rows = SUBLANES
    dk = xs_ref.shape[-1]
    dv = dk
    live = _iota2((rows, 1), 0) < tokens
    ba = ba_ref[0]

    group = _pick_tile(nqk, SAMPLE_HEAD_GROUP, 1)
    nv = group * rep

    def conv(head):
        return _conv_silu(xs_ref, cw_ref[head], rows, lead=(0, head))

    def body(jg, carry):
        heads = [jg * group + i for i in range(group)]
        qs = [_l2norm(conv(h)) * (dk ** -0.5) for h in heads]
        ks = [jnp.where(live, _l2norm(conv(nqk + h)), 0.0) for h in heads]
        kks = [_tiny_nt(k, k) for k in ks]
        qk_raws = [_tiny_nt(q, k) for q, k in zip(qs, ks)]
        cols = [h * rep + e for h in heads for e in range(rep)]
        vs = [conv(2 * nqk + col) for col in cols]
        gates = [_gdn_gates(ba, alog_ref[...], dtb_ref[...], nvh, col) for col in cols]
        betas = [jnp.where(live, b, 0.0) for b, _ in gates]
        gs = [jnp.where(live, g, 0.0) for _, g in gates]
        per_v = lambda xs: [xs[i // rep] for i in range(nv)]
        os_, s_new = _gdn_chunks(per_v(qs), per_v(ks), vs, gs, betas, [s_ref[0, 0, col] for col in cols],
                                 per_v(kks), per_v(qk_raws), tiny=True)
        for i in range(nv):
            sout_ref[0, cols[i]] = s_new[i]
            h = heads[i // rep]
            lanes = pl.ds((i % rep) * dv, dv)
            y = _head_rmsnorm(os_[i], gn_ref[...]) * _silu(z_ref[0, h, :, lanes])
            o_ref[0, h, :, lanes] = y * _sigmoid(ga_ref[0, h, :, lanes])
        return carry

    lax.fori_loop(0, nqk // group, body, 0)


def gdn_sample(xs_h, z_h, ga_h, ba_h, cw_h, a_log, dt_bias, gdn_norm, state, layer, dims):
    nb = xs_h.shape[0]
    nqk, nvh, dk, dv = dims["GDN_QK_HEADS"], dims["GDN_V_HEADS"], dims["GDN_DK"], dims["GDN_DV"]
    assert dk == dv
    rep = nvh // nqk
    full = lambda shape: pl.BlockSpec(shape, lambda bi: (0,) * len(shape))
    per_b = lambda shape: pl.BlockSpec((1,) + shape, lambda bi: (bi,) + (0,) * len(shape))
    in_specs = [
        per_b(xs_h.shape[1:]), per_b(z_h.shape[1:]), per_b(ga_h.shape[1:]), per_b(ba_h.shape[1:]),
        full(cw_h.shape), full((1, nvh)), full((1, nvh)), full((1, dv)),
        pl.BlockSpec((1, 1, nvh, dk, dv), lambda bi: (layer, bi, 0, 0, 0)),
    ]
    out_specs = [per_b((nqk, SUBLANES, rep * dv)), per_b((nvh, dk, dv))]
    out_shape = [jax.ShapeDtypeStruct((nb, nqk, SUBLANES, rep * dv), F32),
                 jax.ShapeDtypeStruct((nb, nvh, dk, dv), F32)]
    return pl.pallas_call(
        functools.partial(_gdn_sample_kernel, nqk=nqk, nvh=nvh, tokens=dims["DEC_T"]),
        grid=(nb,),
        in_specs=in_specs,
        out_specs=out_specs,
        out_shape=out_shape,
        compiler_params=_params("parallel"),
        name="gdn_sample",
    )(xs_h, z_h, ga_h, ba_h, cw_h, a_log[layer][None], dt_bias[layer][None], gdn_norm[layer][None], state)


def _ret_sample_kernel(lg_ref, x_ref, gb_ref, ma_ref, cos_ref, sin_ref, rn_ref, s_ref,
                       o_ref, sout_ref, *, nh, tokens):
    dk = x_ref.shape[-1]
    cos = cos_ref[...]
    sin = sin_ref[...]

    group = _pick_tile(nh, SAMPLE_HEAD_GROUP, 1)

    def body(hg, carry):
        heads = [hg * group + i for i in range(group)]
        qs = [_rotary(x_ref[0, h], cos, sin) for h in heads]
        ks = [_rotary(x_ref[0, nh + h], cos, sin) * (dk ** -0.5) for h in heads]
        vs = [x_ref[0, 2 * nh + h] for h in heads]
        lgs = [jnp.full((1, 1), lg_ref[h], F32) for h in heads]
        os_, s_new = _ret_chunks(qs, ks, vs, [s_ref[0, 0, h] for h in heads], lgs, tokens, tiny=True)
        for i, h in enumerate(heads):
            sout_ref[0, h] = s_new[i]
            y = _head_rmsnorm(os_[i], rn_ref[h]) * _silu(x_ref[0, 3 * nh + h])
            o_ref[0, h] = (ma_ref[0, h] + _sigmoid(gb_ref[0, h]) * y).astype(o_ref.dtype)
        return carry

    lax.fori_loop(0, nh // group, body, 0)


def ret_sample(x_h, gb_h, ma_h, cos, sin, lg, rn_h, state, layer, dims):
    nb = x_h.shape[0]
    nh, dk, dv = dims["RET_HEADS"], dims["RET_DK"], dims["RET_DV"]
    full = lambda shape: pl.BlockSpec(shape, lambda bi: (0,) * len(shape))
    per_b = lambda shape: pl.BlockSpec((1,) + shape, lambda bi: (bi,) + (0,) * len(shape))
    in_specs = [
        pl.BlockSpec(memory_space=pltpu.SMEM),
        per_b(x_h.shape[1:]), per_b(gb_h.shape[1:]), per_b(ma_h.shape[1:]),
        full(cos.shape), full(sin.shape), full(rn_h.shape),
        pl.BlockSpec((1, 1, nh, dk, dv), lambda bi: (layer, bi, 0, 0, 0)),
    ]
    out_specs = [per_b((nh, SUBLANES, dv)), per_b((nh, dk, dv))]
    out_shape = [jax.ShapeDtypeStruct((nb, nh, SUBLANES, dv), BF16),
                 jax.ShapeDtypeStruct((nb, nh, dk, dv), F32)]
    return pl.pallas_call(
        functools.partial(_ret_sample_kernel, nh=nh, tokens=dims["DEC_T"]),
        grid=(nb,),
        in_specs=in_specs,
        out_specs=out_specs,
        out_shape=out_shape,
        compiler_params=_params("parallel"),
        name="ret_sample",
    )(lg, x_h, gb_h, ma_h, cos, sin, rn_h, state)


def _attn_kernel(q_ref, k_ref, v_ref, o_ref, *, heads, lead):
    hd = q_ref.shape[-1] // heads
    for h in range(heads):
        lanes = slice(h * hd, (h + 1) * hd)
        q = q_ref[lead + (slice(None), lanes)]
        kh = k_ref[(0,) * (len(k_ref.shape) - 2) + (slice(None), lanes)]
        vh = v_ref[(0,) * (len(v_ref.shape) - 2) + (slice(None), lanes)]
        s = _dot_nt(q, kh) * (hd ** -0.5)
        m = jnp.max(s, axis=-1, keepdims=True)
        p = jnp.exp(s - m)
        denom = jnp.sum(p, axis=-1, keepdims=True)
        o = _dot(p, vh) / denom
        o_ref[lead + (slice(None), lanes)] = o.astype(o_ref.dtype)


def attn_prompt(q, mk, mv, layer, dims):
    b, t, nm, heads = dims["B"], dims["T"], dims["N_MEM"], dims["MEM_HEADS"]
    w = q.shape[1]
    tq = _pick_tile(t, 512)
    nt = t // tq
    return pl.pallas_call(
        functools.partial(_attn_kernel, heads=heads, lead=()),
        grid=(b, nt),
        in_specs=[pl.BlockSpec((tq, w), lambda bi, i: (bi * nt + i, 0)),
                  pl.BlockSpec((1, nm, w), lambda bi, i: (layer, bi, 0)),
                  pl.BlockSpec((1, nm, w), lambda bi, i: (layer, bi, 0))],
        out_specs=pl.BlockSpec((tq, w), lambda bi, i: (bi * nt + i, 0)),
        out_shape=jax.ShapeDtypeStruct((b * t, w), BF16),
        compiler_params=_params("parallel", "parallel"),
        name="attn_prompt",
    )(q, mk, mv)


def attn_sample(q, mk, mv, layer, dims):
    nb, rows, w = q.shape
    nm, heads = dims["N_MEM"], dims["MEM_HEADS"]
    return pl.pallas_call(
        functools.partial(_attn_kernel, heads=heads, lead=(0,)),
        grid=(nb,),
        in_specs=[pl.BlockSpec((1, rows, w), lambda bi: (bi, 0, 0)),
                  pl.BlockSpec((1, 1, nm, w), lambda bi: (layer, bi, 0, 0)),
                  pl.BlockSpec((1, 1, nm, w), lambda bi: (layer, bi, 0, 0))],
        out_specs=pl.BlockSpec((1, rows, w), lambda bi: (bi, 0, 0)),
        out_shape=jax.ShapeDtypeStruct((nb, rows, w), BF16),
        compiler_params=_params("parallel"),
        name="attn_sample",
    )(q, mk, mv)


def _rope_tables(pos, half):
    inv = 1.0 / (ROPE_BASE ** jnp.linspace(0.0, 1.0, half, dtype=F32))
    ang = pos.astype(F32)[:, None] * inv[None, :]
    return jnp.cos(ang), jnp.sin(ang)


def _head_major(x, width):
    nb, tok, cols = x.shape
    x = jnp.pad(x, ((0, 0), (0, SUBLANES - tok), (0, 0)))
    return x.reshape(nb, SUBLANES, cols // width, width).transpose(0, 2, 1, 3)


def kernel(x_prompt, x_sample, state_gdn, state_conv, state_ret, cache_mem_k, cache_mem_v, mem_prompt,
           norm_mix, w_in, conv_w, a_log, dt_bias, gdn_norm, ret_norm, w_out, norm_mem_q, norm_mem_kv,
           w_mq, w_mk, w_mv, w_mo, norm_ffn, w_ff1, w_ff2, norm_final):
    b, t, d = x_prompt.shape
    nb, dec_t, _ = x_sample.shape
    depth = w_in.shape[0]
    _, _, nvh, gdk, gdv = state_gdn.shape
    conv_ch = state_conv.shape[-1]
    v_w = nvh * gdv
    qk_w = (conv_ch - v_w) // 2
    _, _, rh, rdk, rdv = state_ret.shape
    _, _, n_mem, mem_heads, mem_hd = cache_mem_k.shape
    mem_w = mem_heads * mem_hd
    assert dec_t <= SUBLANES and gdk == 128 and gdv == 128 and rdk == rdv
    col_z = conv_ch
    col_ba = col_z + v_w
    col_ret = col_ba + 2 * nvh
    ret_cols = 2 * rh * rdk + 2 * rh * rdv + 2 * d
    dims = dict(B=b, T=t, DEC_T=dec_t, GDN_QK_HEADS=qk_w // gdk, GDN_V_HEADS=nvh, GDN_DK=gdk, GDN_DV=gdv,
                RET_HEADS=rh, RET_DK=rdk, RET_DV=rdv, N_MEM=n_mem, MEM_HEADS=mem_heads,
                GATE_A_COL=2 * rh * rdk + 2 * rh * rdv, GATE_B_COL=2 * rh * rdk + 2 * rh * rdv + d)
    mp = b * t
    ms = nb * dec_t

    w_ba = w_in[:, :, col_ba:col_ret]
    w_b = w_in[:, :, col_ret:].astype(BF16)
    assert w_b.shape[2] == ret_cols

    cos_p, sin_p = _rope_tables(jnp.arange(t, dtype=jnp.int32), rdk // 2)
    pos_s = PAST_LEN + jnp.arange(SUBLANES, dtype=jnp.int32)
    cos_s, sin_s = _rope_tables(pos_s, rdk // 2)
    lg = jnp.log1p(-jnp.exp2(-5.0 - jnp.arange(rh, dtype=F32)))

    mem2d = mem_prompt.reshape(b * n_mem, d)
    pk, pv = [], []
    for l in range(depth):
        mn = rmsnorm(mem2d, norm_mem_kv[l], BF16)
        pk.append(matmul(mn, w_mk, l))
        pv.append(matmul(mn, w_mv, l))
    pk = jnp.stack(pk)
    pv = jnp.stack(pv)
    ck = cache_mem_k.reshape(depth, nb, n_mem, mem_w)
    cv = cache_mem_v.reshape(depth, nb, n_mem, mem_w)
    cw_h = conv_w.reshape(depth, CONV_W, conv_ch // gdk, gdk).transpose(0, 2, 1, 3)
    rn_h = ret_norm.reshape(depth, rh, 1, rdv)

    h = jnp.concatenate([x_prompt.reshape(mp, d), x_sample.reshape(ms, d)], axis=0)
    pg, pc, pr, sg, sc, sr = [], [], [], [], [], []
    for l in range(depth):
        xn = rmsnorm(h, norm_mix[l], BF16)
        pa = matmul(xn, w_in, l, col0=0, ncols=col_ba)
        pba = matmul(xn, w_ba, l)
        pb = matmul(xn, w_b, l)
        ma_p, g_p = gdn_prompt(pa, pba, pb, conv_w, a_log, dt_bias, gdn_norm, l, dims)
        mg_p, r_p = ret_prompt(pb, ma_p, cos_p, sin_p, lg, ret_norm, l, dims)
        pc.append(pa[:mp].reshape(b, t, col_ba)[:, t - CONV_HIST:, :conv_ch])
        pa_s = pa[mp:].reshape(nb, dec_t, col_ba)
        pb_s = pb[mp:].reshape(nb, dec_t, ret_cols)
        u_s = pa_s[:, :, :conv_ch]
        xs = jnp.concatenate([jnp.zeros((nb, SUBLANES - CONV_HIST, conv_ch), F32), state_conv[l], u_s,
                              jnp.zeros((nb, SUBLANES - dec_t, conv_ch), F32)], axis=1)
        xs_h = xs.reshape(nb, 2 * SUBLANES, conv_ch // gdk, gdk).transpose(0, 2, 1, 3)
        vw = (nvh // dims["GDN_QK_HEADS"]) * gdv
        z_h = _head_major(pa_s[:, :, col_z:col_ba], vw)
        ga_h = _head_major(pb_s[:, :, dims["GATE_A_COL"]:dims["GATE_B_COL"]], vw)
        ba_h = jnp.pad(pba[mp:].reshape(nb, dec_t, 2 * nvh), ((0, 0), (0, SUBLANES - dec_t), (0, 0)))
        ma_s, g_s = gdn_sample(xs_h, z_h, ga_h, ba_h, cw_h[l], a_log, dt_bias, gdn_norm, state_gdn, l, dims)
        x_h = _head_major(pb_s[:, :, :dims["GATE_A_COL"]], rdk)
        gb_h = _head_major(pb_s[:, :, dims["GATE_B_COL"]:], rdv)
        assert vw == rdv and dims["GDN_QK_HEADS"] == rh
        mg_s, r_s = ret_sample(x_h, gb_h, ma_s, cos_s, sin_s, lg, rn_h[l], state_ret, l, dims)
        mg_s = mg_s.transpose(0, 2, 1, 3)[:, :dec_t].reshape(ms, d)
        sc.append(u_s[:, dec_t - CONV_HIST:, :])
        h = matmul(jnp.concatenate([mg_p, mg_s], axis=0), w_out, l, epilogue="residual", residual=h)
        hn = rmsnorm(h, norm_mem_q[l], BF16)
        q = matmul(hn, w_mq, l, out_dtype=BF16)
        att_p = attn_prompt(q, pk, pv, l, dims)
        q_s = jnp.pad(q[mp:].reshape(nb, dec_t, mem_w), ((0, 0), (0, SUBLANES - dec_t), (0, 0)))
        att_s = attn_sample(q_s, ck, cv, l, dims)[:, :dec_t].reshape(ms, mem_w)
        h = matmul(jnp.concatenate([att_p, att_s], axis=0), w_mo, l, epilogue="residual", residual=h)
        hn = rmsnorm(h, norm_ffn[l], BF16)
        u2 = matmul(hn, w_ff1, l, out_dtype=BF16, epilogue="relu2")
        h = matmul(u2, w_ff2, l, epilogue="residual", residual=h)
        pg.append(g_p)
        pr.append(r_p)
        sg.append(g_s)
        sr.append(r_s)
    y = rmsnorm(h, norm_final, F32)
    y_prompt = y[:mp].reshape(b, t, d)
    y_sample = y[mp:].reshape(nb, dec_t, d)
    kv_shape = (depth, b, n_mem, mem_heads, mem_hd)
    return (y_prompt, y_sample, jnp.stack(pg), jnp.stack(pc), jnp.stack(pr),
            pk.reshape(kv_shape), pv.reshape(kv_shape), jnp.stack(sg), jnp.stack(sc), jnp.stack(sr))
```
